```python
import math
import jax, jax.numpy as jnp
from jax import lax
import numpy as np

D_MODEL = 1024
BATCH = 8
SEQ = 8192
DEPTH = 1

N_META = 16
MIX_WIDTH = D_MODEL
SSM_WIDTH = MIX_WIDTH // 2
POOL_WIDTH = MIX_WIDTH - SSM_WIDTH
SSM_GROUP_CH = 16
SSM_GROUPS = SSM_WIDTH // SSM_GROUP_CH
SSM_STATE = 64
DT_MIN = 1e-3
DT_MAX = 1e-1
POOL_WINDOWS = (2, 4, 8, 16)
POOL_GROUPS = len(POOL_WINDOWS)
POOL_GROUP = POOL_WIDTH // POOL_GROUPS
D_FF = ((8 * D_MODEL // 3 + 127) // 128) * 128
RMS_EPS = 1e-6

kernel_name = "hymba_s5_poolformer_macaron_layer"


def rms_norm(x, g):
    xf = x.astype(jnp.float32)
    y = xf * lax.rsqrt(jnp.mean(xf * xf, axis=-1, keepdims=True) + RMS_EPS)
    return (y * g.astype(jnp.float32)).astype(x.dtype)


def swiglu(h, w_gate, w_up, w_down):
    return (jax.nn.silu(h @ w_gate) * (h @ w_up)) @ w_down


def _complex_scan_combine(e1, e2):
    a1r, a1i, b1r, b1i = e1
    a2r, a2i, b2r, b2i = e2
    ar = a2r * a1r - a2i * a1i
    ai = a2r * a1i + a2i * a1r
    a2r_b = a2r[:, None]
    a2i_b = a2i[:, None]
    br = a2r_b * b1r - a2i_b * b1i + b2r
    bi = a2r_b * b1i + a2i_b * b1r + b2i
    return (ar, ai, br, bi)


def s5_mixer(u, lam_re, lam_im, log_dt, b_re, b_im, c_re, c_im, d_skip, w_glu):
    Bt, L, _ = u.shape
    ug = u.astype(jnp.float32).reshape(Bt, L, SSM_GROUPS, SSM_GROUP_CH)
    lr = lam_re.astype(jnp.float32)
    li = lam_im.astype(jnp.float32)
    dt = jnp.exp(log_dt.astype(jnp.float32))[:, None]
    decay = jnp.exp(lr * dt)
    ang = li * dt
    a_re = decay * jnp.cos(ang)
    a_im = decay * jnp.sin(ang)
    nr = a_re - 1.0
    den = lr * lr + li * li
    q_re = (nr * lr + a_im * li) / den
    q_im = (a_im * lr - nr * li) / den
    br = b_re.astype(jnp.float32)
    bi = b_im.astype(jnp.float32)
    bb_re = q_re[..., None] * br - q_im[..., None] * bi
    bb_im = q_re[..., None] * bi + q_im[..., None] * br
    bu_re = jnp.einsum('blgh,gnh->lbgn', ug, bb_re)
    bu_im = jnp.einsum('blgh,gnh->lbgn', ug, bb_im)
    A_re = jnp.broadcast_to(a_re, (L,) + a_re.shape)
    A_im = jnp.broadcast_to(a_im, (L,) + a_im.shape)
    _, _, x_re, x_im = lax.associative_scan(_complex_scan_combine, (A_re, A_im, bu_re, bu_im), axis=0)
    y = (jnp.einsum('lbgn,ghn->blgh', x_re, c_re.astype(jnp.float32))
         - jnp.einsum('lbgn,ghn->blgh', x_im, c_im.astype(jnp.float32)))
    y = y + d_skip.astype(jnp.float32).reshape(SSM_GROUPS, SSM_GROUP_CH) * ug
    z = jnp.einsum('blgh,ghk->blgk', jax.nn.gelu(y), w_glu.astype(jnp.float32))
    out = z[..., :SSM_GROUP_CH] * jax.nn.sigmoid(z[..., SSM_GROUP_CH:])
    return out.reshape(Bt, L, SSM_WIDTH).astype(u.dtype)


def pool_mixer(u, pool_w, pool_scale):
    Bt, L, _ = u.shape
    uf = u.astype(jnp.float32)
    cs = jnp.cumsum(uf, axis=1)
    cs = jnp.concatenate([jnp.zeros((Bt, 1, POOL_WIDTH), jnp.float32), cs], axis=1)
    t1 = jnp.arange(1, L + 1, dtype=jnp.float32)
    outs = []
    for gi, w in enumerate(POOL_WINDOWS):
        lo_c, hi_c = gi * POOL_GROUP, (gi + 1) * POOL_GROUP
        c = cs[:, :, lo_c:hi_c]
        hi = c[:, 1:]
        lo = jnp.concatenate([jnp.zeros((Bt, w - 1, POOL_GROUP), jnp.float32), c[:, :L - w + 1]], axis=1)
        cnt = jnp.minimum(t1, float(w))[None, :, None]
        outs.append((hi - lo) / cnt - uf[:, :, lo_c:hi_c])
    pooled = jnp.stack(outs, axis=2)
    mixed = jnp.einsum('blgc,gcd->blgd', pooled, pool_w.astype(jnp.float32))
    mixed = mixed * pool_scale.astype(jnp.float32).reshape(POOL_GROUPS, POOL_GROUP)
    return mixed.reshape(Bt, L, POOL_WIDTH).astype(u.dtype)


def setup_inputs(seed: int = 0) -> dict:
    key = jax.random.key(seed)
    ks = jax.random.split(key, 40)
    f32 = jnp.float32

    def nrm(k, shape, scale):
        return jax.random.normal(k, shape, f32) * scale

    def gain(k, shape):
        return 1.0 + 0.02 * jax.random.normal(k, shape, f32)

    Dp = DEPTH
    G, N, H = SSM_GROUPS, SSM_STATE, SSM_GROUP_CH
    n_idx = jnp.arange(N, dtype=f32)
    return {
        "x": nrm(ks[0], (BATCH, SEQ, D_MODEL), 1.0),
        "meta_tokens": nrm(ks[1], (N_META, D_MODEL), 1.0),
        "ffn1_pre_norm": gain(ks[2], (Dp, D_MODEL)),
        "ffn1_post_norm": gain(ks[3], (Dp, D_MODEL)),
        "ffn1_w_gate": nrm(ks[4], (Dp, D_MODEL, D_FF), D_MODEL ** -0.5),
        "ffn1_w_up": nrm(ks[5], (Dp, D_MODEL, D_FF), D_MODEL ** -0.5),
        "ffn1_w_down": nrm(ks[6], (Dp, D_FF, D_MODEL), D_FF ** -0.5),
        "mix_pre_norm": gain(ks[7], (Dp, D_MODEL)),
        "mix_post_norm": gain(ks[8], (Dp, D_MODEL)),
        "w_in": nrm(ks[9], (Dp, D_MODEL, MIX_WIDTH), D_MODEL ** -0.5),
        "ssm_lambda_re": -0.5 + 0.01 * jax.random.normal(ks[10], (Dp, G, N), f32),
        "ssm_lambda_im": math.pi * n_idx + 0.01 * jax.random.normal(ks[11], (Dp, G, N), f32),
        "ssm_log_dt": jax.random.uniform(ks[12], (Dp, G), f32, math.log(DT_MIN), math.log(DT_MAX)),
        "ssm_b_re": nrm(ks[13], (Dp, G, N, H), (2.0 * H) ** -0.5),
        "ssm_b_im": nrm(ks[14], (Dp, G, N, H), (2.0 * H) ** -0.5),
        "ssm_c_re": nrm(ks[15], (Dp, G, H, N), (2.0 * N) ** -0.5),
        "ssm_c_im": nrm(ks[16], (Dp, G, H, N), (2.0 * N) ** -0.5),
        "ssm_d": nrm(ks[17], (Dp, SSM_WIDTH), 1.0),
        "ssm_w_glu": nrm(ks[18], (Dp, G, H, 2 * H), H ** -0.5),
        "pool_w": nrm(ks[19], (Dp, POOL_GROUPS, POOL_GROUP, POOL_GROUP), POOL_GROUP ** -0.5),
        "pool_scale": gain(ks[20], (Dp, POOL_WIDTH)),
        "ssm_out_norm": gain(ks[21], (Dp, SSM_WIDTH)),
        "pool_out_norm": gain(ks[22], (Dp, POOL_WIDTH)),
        "w_out": nrm(ks[23], (Dp, MIX_WIDTH, D_MODEL), MIX_WIDTH ** -0.5),
        "ffn2_pre_norm": gain(ks[24], (Dp, D_MODEL)),
        "ffn2_post_norm": gain(ks[25], (Dp, D_MODEL)),
        "ffn2_w_gate": nrm(ks[26], (Dp, D_MODEL, D_FF), D_MODEL ** -0.5),
        "ffn2_w_up": nrm(ks[27], (Dp, D_MODEL, D_FF), D_MODEL ** -0.5),
        "ffn2_w_down": nrm(ks[28], (Dp, D_FF, D_MODEL), D_FF ** -0.5),
    }


def reference(x, meta_tokens, ffn1_pre_norm, ffn1_post_norm, ffn1_w_gate, ffn1_w_up, ffn1_w_down,
              mix_pre_norm, mix_post_norm, w_in, ssm_lambda_re, ssm_lambda_im, ssm_log_dt,
              ssm_b_re, ssm_b_im, ssm_c_re, ssm_c_im, ssm_d, ssm_w_glu, pool_w, pool_scale,
              ssm_out_norm, pool_out_norm, w_out, ffn2_pre_norm, ffn2_post_norm,
              ffn2_w_gate, ffn2_w_up, ffn2_w_down):
    Bt = x.shape[0]
    meta = jnp.broadcast_to(meta_tokens.astype(x.dtype)[None], (Bt, N_META, D_MODEL))
    h = jnp.concatenate([meta, x], axis=1)
    for i in range(DEPTH):
        f = swiglu(rms_norm(h, ffn1_pre_norm[i]), ffn1_w_gate[i], ffn1_w_up[i], ffn1_w_down[i])
        h = h + 0.5 * rms_norm(f, ffn1_post_norm[i])
        proj = rms_norm(h, mix_pre_norm[i]) @ w_in[i]
        y_ssm = s5_mixer(proj[..., :SSM_WIDTH], ssm_lambda_re[i], ssm_lambda_im[i], ssm_log_dt[i],
                         ssm_b_re[i], ssm_b_im[i], ssm_c_re[i], ssm_c_im[i], ssm_d[i], ssm_w_glu[i])
        y_pool = pool_mixer(proj[..., SSM_WIDTH:], pool_w[i], pool_scale[i])
        mixed = jnp.concatenate([rms_norm(y_ssm, ssm_out_norm[i]),
                                 rms_norm(y_pool, pool_out_norm[i])], axis=-1) @ w_out[i]
        h = h + rms_norm(mixed, mix_post_norm[i])
        f = swiglu(rms_norm(h, ffn2_pre_norm[i]), ffn2_w_gate[i], ffn2_w_up[i], ffn2_w_down[i])
        h = h + 0.5 * rms_norm(f, ffn2_post_norm[i])
    return h[:, N_META:]
```

```python
import functools
import math

import jax
import jax.numpy as jnp
from jax import lax
from jax.experimental import pallas as pl
from jax.experimental.pallas import tpu as pltpu

D_MODEL = 1024
N_META = 16
SSM_WIDTH = 512
POOL_WIDTH = 512
SSM_GROUP_CH = 16
SSM_GROUPS = SSM_WIDTH // SSM_GROUP_CH
SSM_STATE = 64
N_STATE = SSM_GROUPS * SSM_STATE
POOL_WINDOWS = (2, 4, 8, 16)
POOL_GROUP = POOL_WIDTH // len(POOL_WINDOWS)
D_FF = 2816
RMS_EPS = 1e-6

F32_SUBLANES = 8
VMEM_LIMIT_BYTES = 56 * 1024 * 1024

FFN_ROWS = 512
FFN_CHUNK = 256
MIX_TL = 64
SCAN_LANES = 1024

f32 = jnp.float32
bf16 = jnp.bfloat16


def _rms(x, g):
    return x * lax.rsqrt(jnp.mean(x * x, axis=-1, keepdims=True) + RMS_EPS) * g


def _dot(a, b):
    return jnp.dot(a, b, preferred_element_type=f32)


def _ffn_kernel(x_ref, pre_ref, post_ref, wg_ref, wu_ref, wd_ref, o_ref, acc_ref):
    x = x_ref[...]
    n = _rms(x, pre_ref[...]).astype(bf16)
    for c in range(D_FF // FFN_CHUNK):
        cols = slice(c * FFN_CHUNK, (c + 1) * FFN_CHUNK)
        gate = _dot(n, wg_ref[:, cols])
        up = _dot(n, wu_ref[:, cols])
        act = (gate * jax.nn.sigmoid(gate) * up).astype(bf16)
        part = _dot(act, wd_ref[cols, :])
        if c == 0:
            acc_ref[...] = part
        else:
            acc_ref[...] += part
    o_ref[...] = x + 0.5 * _rms(acc_ref[...], post_ref[...])


def _const_spec(shape):
    zeros = (0,) * len(shape)
    return pl.BlockSpec(shape, lambda *_: zeros, pipeline_mode=pl.Buffered(1))


def _ffn(x, grid, in_spec, out_spec, out_shape, rows, pre, post, wg, wu, wd):
    return pl.pallas_call(
        _ffn_kernel,
        grid=grid,
        in_specs=[
            in_spec,
            _const_spec((1, D_MODEL)),
            _const_spec((1, D_MODEL)),
            _const_spec((D_MODEL, D_FF)),
            _const_spec((D_MODEL, D_FF)),
            _const_spec((D_FF, D_MODEL)),
        ],
        out_specs=out_spec,
        out_shape=jax.ShapeDtypeStruct(out_shape, f32),
        scratch_shapes=[pltpu.VMEM((rows, D_MODEL), f32)],
        compiler_params=pltpu.CompilerParams(
            dimension_semantics=("arbitrary",) * len(grid),
            vmem_limit_bytes=VMEM_LIMIT_BYTES),
        name="ffn",
    )(x, pre, post, wg, wu, wd)


def _gelu_tanh(y):
    return 0.5 * y * (1.0 + jnp.tanh(math.sqrt(2.0 / math.pi) * (y + 0.044715 * (y * y * y))))


def _mixer_kernel(h_ref, hm_ref, gpre_ref, win_ref, bre_ref, bim_ref, are_ref, aim_ref,
                  cre_ref, cimn_ref, d_ref, wglu_ref, gssm_ref, pw_ref, pscale_ref, gpool_ref,
                  wout_ref, gpost_ref, o_ref,
                  xr_ref, xi_ref, bur_ref, bui_ref, xsr_ref, xsi_ref, ext_ref):
    rows = MIX_TL * F32_SUBLANES
    hist = N_META * F32_SUBLANES

    def project(h):
        return _dot(_rms(h, gpre_ref[...]).astype(bf16), win_ref[...])

    def drive(u_ssm, n):
        ub = u_ssm.astype(bf16)
        bur_ref[0:n, :] = _dot(ub, bre_ref[...])
        bui_ref[0:n, :] = _dot(ub, bim_ref[...])

    def scan(steps):
        for c in range(N_STATE // SCAN_LANES):
            lanes = slice(c * SCAN_LANES, (c + 1) * SCAN_LANES)
            ar = are_ref[:, lanes]
            ai = aim_ref[:, lanes]

            def body(tp, carry):
                xr, xi = carry
                r0 = pl.multiple_of(tp * (2 * F32_SUBLANES), 2 * F32_SUBLANES)
                r1 = r0 + F32_SUBLANES
                xr1 = ar * xr - ai * xi + bur_ref[pl.ds(r0, F32_SUBLANES), lanes]
                xi1 = ar * xi + ai * xr + bui_ref[pl.ds(r0, F32_SUBLANES), lanes]
                xr2 = ar * xr1 - ai * xi1 + bur_ref[pl.ds(r1, F32_SUBLANES), lanes]
                xi2 = ar * xi1 + ai * xr1 + bui_ref[pl.ds(r1, F32_SUBLANES), lanes]
                xsr_ref[pl.ds(r0, 2 * F32_SUBLANES), lanes] = (
                    jnp.concatenate([xr1, xr2], axis=0).astype(bf16))
                xsi_ref[pl.ds(r0, 2 * F32_SUBLANES), lanes] = (
                    jnp.concatenate([xi1, xi2], axis=0).astype(bf16))
                return xr2, xi2

            xr, xi = lax.fori_loop(0, steps // 2, body, (xr_ref[:, lanes], xi_ref[:, lanes]))
            xr_ref[:, lanes] = xr
            xi_ref[:, lanes] = xi

    @pl.when(pl.program_id(0) == 0)
    def _():
        xr_ref[...] = jnp.zeros_like(xr_ref)
        xi_ref[...] = jnp.zeros_like(xi_ref)
        pm = project(hm_ref[...])
        drive(pm[:, :SSM_WIDTH], hist)
        scan(N_META)
        ext_ref[0:hist, :] = pm[:, SSM_WIDTH:]

    h = h_ref[...]
    p = project(h)
    u_ssm = p[:, :SSM_WIDTH]

    drive(u_ssm, rows)
    scan(MIX_TL)
    y = _dot(xsr_ref[...], cre_ref[...]) + _dot(xsi_ref[...], cimn_ref[...]) + d_ref[...] * u_ssm
    z = _dot(_gelu_tanh(y).astype(bf16), wglu_ref[...])
    s = z[:, :SSM_WIDTH] * jax.nn.sigmoid(z[:, SSM_WIDTH:])
    ssm_n = _rms(s, gssm_ref[...]).astype(bf16)

    ext_ref[hist:hist + rows, :] = p[:, SSM_WIDTH:]
    pooled_groups = []
    for gi, w in enumerate(POOL_WINDOWS):
        lanes = slice(gi * POOL_GROUP, (gi + 1) * POOL_GROUP)
        cur = ext_ref[hist:hist + rows, lanes]
        acc = cur
        for k in range(1, w):
            lo = hist - k * F32_SUBLANES
            acc = acc + ext_ref[lo:lo + rows, lanes]
        pooled = acc * (1.0 / w) - cur
        pooled_groups.append(_dot(pooled.astype(bf16), pw_ref[gi]) * pscale_ref[:, lanes])
    ext_ref[0:hist, :] = ext_ref[rows:rows + hist, :]
    pool_n = _rms(jnp.concatenate(pooled_groups, axis=1), gpool_ref[...]).astype(bf16)

    mixed = _dot(ssm_n, wout_ref[0:SSM_WIDTH, :]) + _dot(pool_n, wout_ref[SSM_WIDTH:, :])
    o_ref[...] = h + _rms(mixed, gpost_ref[...])


def _mixer(h1t, h1m, consts, n_steps):
    rows = MIX_TL * F32_SUBLANES
    hist = N_META * F32_SUBLANES
    tile = pl.BlockSpec((rows, D_MODEL), lambda i: (i, 0))
    return pl.pallas_call(
        _mixer_kernel,
        grid=(n_steps,),
        in_specs=[tile, _const_spec(h1m.shape)] + [_const_spec(c.shape) for c in consts],
        out_specs=tile,
        out_shape=jax.ShapeDtypeStruct(h1t.shape, f32),
        scratch_shapes=[
            pltpu.VMEM((F32_SUBLANES, N_STATE), f32),
            pltpu.VMEM((F32_SUBLANES, N_STATE), f32),
            pltpu.VMEM((rows, N_STATE), f32),
            pltpu.VMEM((rows, N_STATE), f32),
            pltpu.VMEM((rows, N_STATE), bf16),
            pltpu.VMEM((rows, N_STATE), bf16),
            pltpu.VMEM((hist + rows, POOL_WIDTH), f32),
        ],
        compiler_params=pltpu.CompilerParams(
            dimension_semantics=("arbitrary",),
            vmem_limit_bytes=VMEM_LIMIT_BYTES),
        name="mixer",
    )(h1t, h1m, *consts)


def _ssm_constants(lam_re, lam_im, log_dt, b_re, b_im, c_re, c_im, d_skip, w_glu):
    G, N, H = SSM_GROUPS, SSM_STATE, SSM_GROUP_CH
    dt = jnp.exp(log_dt)[:, None]
    decay = jnp.exp(lam_re * dt)
    ang = lam_im * dt
    a_re = decay * jnp.cos(ang)
    a_im = decay * jnp.sin(ang)
    nr = a_re - 1.0
    den = lam_re * lam_re + lam_im * lam_im
    q_re = (nr * lam_re + a_im * lam_im) / den
    q_im = (a_im * lam_re - nr * lam_im) / den
    bb_re = q_re[..., None] * b_re - q_im[..., None] * b_im
    bb_im = q_re[..., None] * b_im + q_im[..., None] * b_re
    eye = jnp.eye(G, dtype=f32)
    bre = jnp.einsum('gnh,gk->ghkn', bb_re, eye).reshape(G * H, G * N).astype(bf16)
    bim = jnp.einsum('gnh,gk->ghkn', bb_im, eye).reshape(G * H, G * N).astype(bf16)
    cre = jnp.einsum('ghn,gk->gnkh', c_re, eye).reshape(G * N, G * H).astype(bf16)
    cimn = jnp.einsum('ghn,gk->gnkh', -c_im, eye).reshape(G * N, G * H).astype(bf16)
    wv = jnp.einsum('ghj,gk->ghkj', w_glu[..., :H], eye).reshape(G * H, G * H)
    wgate = jnp.einsum('ghj,gk->ghkj', w_glu[..., H:], eye).reshape(G * H, G * H)
    wglu = jnp.concatenate([wv, wgate], axis=1).astype(bf16)
    are = jnp.broadcast_to(a_re.reshape(1, G * N), (F32_SUBLANES, G * N))
    aim = jnp.broadcast_to(a_im.reshape(1, G * N), (F32_SUBLANES, G * N))
    return bre, bim, are, aim, cre, cimn, d_skip.reshape(1, SSM_WIDTH), wglu


def kernel(x, meta_tokens, ffn1_pre_norm, ffn1_post_norm, ffn1_w_gate, ffn1_w_up, ffn1_w_down,
           mix_pre_norm, mix_post_norm, w_in, ssm_lambda_re, ssm_lambda_im, ssm_log_dt,
           ssm_b_re, ssm_b_im, ssm_c_re, ssm_c_im, ssm_d, ssm_w_glu, pool_w, pool_scale,
           ssm_out_norm, pool_out_norm, w_out, ffn2_pre_norm, ffn2_post_norm,
           ffn2_w_gate, ffn2_w_up, ffn2_w_down):
    B, L, D = x.shape
    assert D == D_MODEL and B == F32_SUBLANES and L % FFN_ROWS == 0 and L % MIX_TL == 0
    assert ffn1_pre_norm.shape[0] == 1, "single layer"
    assert N_META >= max(POOL_WINDOWS)
    row = lambda v: v.reshape(1, -1).astype(f32)

    ffn1 = (row(ffn1_pre_norm), row(ffn1_post_norm), ffn1_w_gate[0].astype(bf16),
            ffn1_w_up[0].astype(bf16), ffn1_w_down[0].astype(bf16))
    ffn2 = (row(ffn2_pre_norm), row(ffn2_post_norm), ffn2_w_gate[0].astype(bf16),
            ffn2_w_up[0].astype(bf16), ffn2_w_down[0].astype(bf16))

    batch_major = pl.BlockSpec((None, FFN_ROWS, D), lambda b, i: (b, i, 0))
    time_major = pl.BlockSpec((FFN_ROWS, D), lambda b, i: (i, b))
    grid = (B, L // FFN_ROWS)

    h1t = _ffn(x, grid, batch_major, time_major, (L, B * D), FFN_ROWS, *ffn1)
    whole = pl.BlockSpec((N_META, D), lambda i: (0, 0))
    h1m = _ffn(meta_tokens.astype(f32), (1,), whole, whole, (N_META, D), N_META, *ffn1)
    h1m = jnp.repeat(h1m, B, axis=0)

    consts = (row(mix_pre_norm), w_in[0].astype(bf16)) + _ssm_constants(
        ssm_lambda_re[0], ssm_lambda_im[0], ssm_log_dt[0], ssm_b_re[0], ssm_b_im[0],
        ssm_c_re[0], ssm_c_im[0], ssm_d[0], ssm_w_glu[0]) + (
        row(ssm_out_norm), pool_w[0].astype(bf16), row(pool_scale), row(pool_out_norm),
        w_out[0].astype(bf16), row(mix_post_norm))
    h2t = _mixer(h1t.reshape(L * B, D), h1m, consts, L // MIX_TL).reshape(L, B * D)

    return _ffn(h2t, grid, time_major, batch_major, (B, L, D), FFN_ROWS, *ffn2)
```

```python
import functools
import math

import jax
import jax.numpy as jnp
from jax import lax
from jax.experimental import pallas as pl
from jax.experimental.pallas import tpu as pltpu

D_MODEL = 1024
N_META = 16
SSM_WIDTH = 512
POOL_WIDTH = 512
SSM_GROUP_CH = 16
SSM_GROUPS = SSM_WIDTH // SSM_GROUP_CH
SSM_STATE = 64
N_STATE = SSM_GROUPS * SSM_STATE
POOL_WINDOWS = (2, 4, 8, 16)
POOL_GROUP = POOL_WIDTH // len(POOL_WINDOWS)
D_FF = 2816
RMS_EPS = 1e-6

F32_SUBLANES = 8
LANES = 128
VMEM_LIMIT_BYTES = 56 * 1024 * 1024

FFN_TT = 64
FFN_CHUNK = 256
MIX_TL = 64
SCAN_LANES = 1024

f32 = jnp.float32
bf16 = jnp.bfloat16


def _rms(x, g):
    return x * lax.rsqrt(jnp.mean(x * x, axis=-1, keepdims=True) + RMS_EPS) * g


def _dot(a, b):
    return jnp.dot(a, b, preferred_element_type=f32)


def _load_rows(x_ref, slabs):
    if not slabs:
        return x_ref[...].reshape(-1, D_MODEL)
    tt = x_ref.shape[1] // F32_SUBLANES
    return jnp.concatenate(
        [jnp.concatenate([x_ref.at[j][pl.ds(b, tt, stride=F32_SUBLANES), :]
                          for j in range(D_MODEL // LANES)], axis=1)
         for b in range(F32_SUBLANES)], axis=0)


def _store_rows(o_ref, val, slabs):
    if not slabs:
        o_ref[...] = val.reshape(o_ref.shape)
        return
    tt = o_ref.shape[1] // F32_SUBLANES
    for b in range(F32_SUBLANES):
        for j in range(D_MODEL // LANES):
            o_ref.at[j][pl.ds(b, tt, stride=F32_SUBLANES), :] = (
                val[b * tt:(b + 1) * tt, j * LANES:(j + 1) * LANES])


def _ffn_kernel(x_ref, pre_ref, post_ref, wg_ref, wu_ref, wd_ref, o_ref, acc_ref, *,
                slabs_in, slabs_out):
    x = _load_rows(x_ref, slabs_in)
    n = _rms(x, pre_ref[...]).astype(bf16)
    for c in range(D_FF // FFN_CHUNK):
        cols = slice(c * FFN_CHUNK, (c + 1) * FFN_CHUNK)
        gate = _dot(n, wg_ref[:, cols])
        up = _dot(n, wu_ref[:, cols])
        act = (gate * jax.nn.sigmoid(gate) * up).astype(bf16)
        part = _dot(act, wd_ref[cols, :])
        if c == 0:
            acc_ref[...] = part
        else:
            acc_ref[...] += part
    _store_rows(o_ref, x + 0.5 * _rms(acc_ref[...], post_ref[...]), slabs_out)


def _const_spec(shape):
    zeros = (0,) * len(shape)
    return pl.BlockSpec(shape, lambda *_: zeros, pipeline_mode=pl.Buffered(1))


def _ffn(x, grid, in_spec, out_spec, out_shape, rows, pre, post, wg, wu, wd, *,
         slabs_in=False, slabs_out=False):
    return pl.pallas_call(
        functools.partial(_ffn_kernel, slabs_in=slabs_in, slabs_out=slabs_out),
        grid=grid,
        in_specs=[
            in_spec,
            _const_spec((1, D_MODEL)),
            _const_spec((1, D_MODEL)),
            _const_spec((D_MODEL, D_FF)),
            _const_spec((D_MODEL, D_FF)),
            _const_spec((D_FF, D_MODEL)),
        ],
        out_specs=out_spec,
        out_shape=jax.ShapeDtypeStruct(out_shape, f32),
        scratch_shapes=[pltpu.VMEM((rows, D_MODEL), f32)],
        compiler_params=pltpu.CompilerParams(
            dimension_semantics=("arbitrary",) * len(grid),
            vmem_limit_bytes=VMEM_LIMIT_BYTES),
        name="ffn",
    )(x, pre, post, wg, wu, wd)


def _gelu_tanh(y):
    return 0.5 * y * (1.0 + jnp.tanh(math.sqrt(2.0 / math.pi) * (y + 0.044715 * (y * y * y))))


def _mixer_kernel(h_ref, hm_ref, gpre_ref, win_ref, bre_ref, bim_ref, are_ref, aim_ref,
                  cre_ref, cimn_ref, d_ref, wglu_ref, gssm_ref, pw_ref, pscale_ref, gpool_ref,
                  wout_ref, gpost_ref, o_ref,
                  xr_ref, xi_ref, bur_ref, bui_ref, xsr_ref, xsi_ref, ext_ref):
    rows = MIX_TL * F32_SUBLANES
    hist = N_META * F32_SUBLANES

    def project(h):
        return _dot(_rms(h, gpre_ref[...]).astype(bf16), win_ref[...])

    def drive(u_ssm, n):
        ub = u_ssm.astype(bf16)
        bur_ref[0:n, :] = _dot(ub, bre_ref[...])
        bui_ref[0:n, :] = _dot(ub, bim_ref[...])

    def scan(steps):
        for c in range(N_STATE // SCAN_LANES):
            lanes = slice(c * SCAN_LANES, (c + 1) * SCAN_LANES)
            ar = are_ref[:, lanes]
            ai = aim_ref[:, lanes]

            def body(tp, carry):
                xr, xi = carry
                r0 = pl.multiple_of(tp * (2 * F32_SUBLANES), 2 * F32_SUBLANES)
                r1 = r0 + F32_SUBLANES
                xr1 = ar * xr - ai * xi + bur_ref[pl.ds(r0, F32_SUBLANES), lanes]
                xi1 = ar * xi + ai * xr + bui_ref[pl.ds(r0, F32_SUBLANES), lanes]
                xr2 = ar * xr1 - ai * xi1 + bur_ref[pl.ds(r1, F32_SUBLANES), lanes]
                xi2 = ar * xi1 + ai * xr1 + bui_ref[pl.ds(r1, F32_SUBLANES), lanes]
                xsr_ref[pl.ds(r0, 2 * F32_SUBLANES), lanes] = (
                    jnp.concatenate([xr1, xr2], axis=0).astype(bf16))
                xsi_ref[pl.ds(r0, 2 * F32_SUBLANES), lanes] = (
                    jnp.concatenate([xi1, xi2], axis=0).astype(bf16))
                return xr2, xi2

            xr, xi = lax.fori_loop(0, steps // 2, body, (xr_ref[:, lanes], xi_ref[:, lanes]))
            xr_ref[:, lanes] = xr
            xi_ref[:, lanes] = xi

    @pl.when(pl.program_id(0) == 0)
    def _():
        xr_ref[...] = jnp.zeros_like(xr_ref)
        xi_ref[...] = jnp.zeros_like(xi_ref)
        pm = project(hm_ref[...])
        drive(pm[:, :SSM_WIDTH], hist)
        scan(N_META)
        ext_ref[0:hist, :] = pm[:, SSM_WIDTH:]

    h = jnp.concatenate([h_ref[j] for j in range(D_MODEL // LANES)], axis=1)
    p = project(h)
    u_ssm = p[:, :SSM_WIDTH]

    drive(u_ssm, rows)
    scan(MIX_TL)
    y = _dot(xsr_ref[...], cre_ref[...]) + _dot(xsi_ref[...], cimn_ref[...]) + d_ref[...] * u_ssm
    z = _dot(_gelu_tanh(y).astype(bf16), wglu_ref[...])
    s = z[:, :SSM_WIDTH] * jax.nn.sigmoid(z[:, SSM_WIDTH:])
    ssm_n = _rms(s, gssm_ref[...]).astype(bf16)

    ext_ref[hist:hist + rows, :] = p[:, SSM_WIDTH:]
    pooled_groups = []
    for gi, w in enumerate(POOL_WINDOWS):
        lanes = slice(gi * POOL_GROUP, (gi + 1) * POOL_GROUP)
        cur = ext_ref[hist:hist + rows, lanes]
        acc = cur
        for k in range(1, w):
            lo = hist - k * F32_SUBLANES
            acc = acc + ext_ref[lo:lo + rows, lanes]
        pooled = acc * (1.0 / w) - cur
        pooled_groups.append(_dot(pooled.astype(bf16), pw_ref[gi]) * pscale_ref[:, lanes])
    ext_ref[0:hist, :] = ext_ref[rows:rows + hist, :]
    pool_n = _rms(jnp.concatenate(pooled_groups, axis=1), gpool_ref[...]).astype(bf16)

    mixed = _dot(ssm_n, wout_ref[0:SSM_WIDTH, :]) + _dot(pool_n, wout_ref[SSM_WIDTH:, :])
    out = h + _rms(mixed, gpost_ref[...])
    for j in range(D_MODEL // LANES):
        o_ref[j] = out[:, j * LANES:(j + 1) * LANES]


def _mixer(h1t, h1m, consts, n_steps):
    rows = MIX_TL * F32_SUBLANES
    hist = N_META * F32_SUBLANES
    tile = pl.BlockSpec((D_MODEL // LANES, rows, LANES), lambda i: (0, i, 0))
    return pl.pallas_call(
        _mixer_kernel,
        grid=(n_steps,),
        in_specs=[tile, _const_spec(h1m.shape)] + [_const_spec(c.shape) for c in consts],
        out_specs=tile,
        out_shape=jax.ShapeDtypeStruct(h1t.shape, f32),
        scratch_shapes=[
            pltpu.VMEM((F32_SUBLANES, N_STATE), f32),
            pltpu.VMEM((F32_SUBLANES, N_STATE), f32),
            pltpu.VMEM((rows, N_STATE), f32),
            pltpu.VMEM((rows, N_STATE), f32),
            pltpu.VMEM((rows, N_STATE), bf16),
            pltpu.VMEM((rows, N_STATE), bf16),
            pltpu.VMEM((hist + rows, POOL_WIDTH), f32),
        ],
        compiler_params=pltpu.CompilerParams(
            dimension_semantics=("arbitrary",),
            vmem_limit_bytes=VMEM_LIMIT_BYTES),
        name="mixer",
    )(h1t, h1m, *consts)


def _ssm_constants(lam_re, lam_im, log_dt, b_re, b_im, c_re, c_im, d_skip, w_glu):
    G, N, H = SSM_GROUPS, SSM_STATE, SSM_GROUP_CH
    dt = jnp.exp(log_dt)[:, None]
    decay = jnp.exp(lam_re * dt)
    ang = lam_im * dt
    a_re = decay * jnp.cos(ang)
    a_im = decay * jnp.sin(ang)
    nr = a_re - 1.0
    den = lam_re * lam_re + lam_im * lam_im
    q_re = (nr * lam_re + a_im * lam_im) / den
    q_im = (a_im * lam_re - nr * lam_im) / den
    bb_re = q_re[..., None] * b_re - q_im[..., None] * b_im
    bb_im = q_re[..., None] * b_im + q_im[..., None] * b_re
    eye = jnp.eye(G, dtype=f32)
    bre = jnp.einsum('gnh,gk->ghkn', bb_re, eye).reshape(G * H, G * N).astype(bf16)
    bim = jnp.einsum('gnh,gk->ghkn', bb_im, eye).reshape(G * H, G * N).astype(bf16)
    cre = jnp.einsum('ghn,gk->gnkh', c_re, eye).reshape(G * N, G * H).astype(bf16)
    cimn = jnp.einsum('ghn,gk->gnkh', -c_im, eye).reshape(G * N, G * H).astype(bf16)
    wv = jnp.einsum('ghj,gk->ghkj', w_glu[..., :H], eye).reshape(G * H, G * H)
    wgate = jnp.einsum('ghj,gk->ghkj', w_glu[..., H:], eye).reshape(G * H, G * H)
    wglu = jnp.concatenate([wv, wgate], axis=1).astype(bf16)
    are = jnp.broadcast_to(a_re.reshape(1, G * N), (F32_SUBLANES, G * N))
    aim = jnp.broadcast_to(a_im.reshape(1, G * N), (F32_SUBLANES, G * N))
    return bre, bim, are, aim, cre, cimn, d_skip.reshape(1, SSM_WIDTH), wglu


def kernel(x, meta_tokens, ffn1_pre_norm, ffn1_post_norm, ffn1_w_gate, ffn1_w_up, ffn1_w_down,
           mix_pre_norm, mix_post_norm, w_in, ssm_lambda_re, ssm_lambda_im, ssm_log_dt,
           ssm_b_re, ssm_b_im, ssm_c_re, ssm_c_im, ssm_d, ssm_w_glu, pool_w, pool_scale,
           ssm_out_norm, pool_out_norm, w_out, ffn2_pre_norm, ffn2_post_norm,
           ffn2_w_gate, ffn2_w_up, ffn2_w_down):
    B, L, D = x.shape
    assert D == D_MODEL and B == F32_SUBLANES and L % FFN_TT == 0 and L % MIX_TL == 0
    assert ffn1_pre_norm.shape[0] == 1, "single layer"
    assert N_META >= max(POOL_WINDOWS)
    row = lambda v: v.reshape(1, -1).astype(f32)

    ffn1 = (row(ffn1_pre_norm), row(ffn1_post_norm), ffn1_w_gate[0].astype(bf16),
            ffn1_w_up[0].astype(bf16), ffn1_w_down[0].astype(bf16))
    ffn2 = (row(ffn2_pre_norm), row(ffn2_post_norm), ffn2_w_gate[0].astype(bf16),
            ffn2_w_up[0].astype(bf16), ffn2_w_down[0].astype(bf16))

    rows = FFN_TT * B
    grid = (L // FFN_TT,)
    batch_major = pl.BlockSpec((B, FFN_TT, D), lambda i: (0, i, 0))
    slab_shape = (D // LANES, L * B, LANES)
    time_major = pl.BlockSpec((D // LANES, rows, LANES), lambda i: (0, i, 0))

    h1t = _ffn(x, grid, batch_major, time_major, slab_shape, rows, *ffn1, slabs_out=True)
    whole = pl.BlockSpec((N_META, D), lambda i: (0, 0))
    h1m = _ffn(meta_tokens.astype(f32), (1,), whole, whole, (N_META, D), N_META, *ffn1)
    h1m = jnp.repeat(h1m, B, axis=0)

    consts = (row(mix_pre_norm), w_in[0].astype(bf16)) + _ssm_constants(
        ssm_lambda_re[0], ssm_lambda_im[0], ssm_log_dt[0], ssm_b_re[0], ssm_b_im[0],
        ssm_c_re[0], ssm_c_im[0], ssm_d[0], ssm_w_glu[0]) + (
        row(ssm_out_norm), pool_w[0].astype(bf16), row(pool_scale), row(pool_out_norm),
        w_out[0].astype(bf16), row(mix_post_norm))
    h2t = _mixer(h1t, h1m, consts, L // MIX_TL)

    return _ffn(h2t, grid, time_major, batch_major, (B, L, D), rows, *ffn2, slabs_in=True)
```

```python
import functools
import math

import jax
import jax.numpy as jnp
from jax import lax
from jax.experimental import pallas as pl
from jax.experimental.pallas import tpu as pltpu

D_MODEL = 1024
N_META = 16
SSM_WIDTH = 512
POOL_WIDTH = 512
SSM_GROUP_CH = 16
SSM_GROUPS = SSM_WIDTH // SSM_GROUP_CH
SSM_STATE = 64
N_STATE = SSM_GROUPS * SSM_STATE
POOL_WINDOWS = (2, 4, 8, 16)
POOL_GROUP = POOL_WIDTH // len(POOL_WINDOWS)
D_FF = 2816
RMS_EPS = 1e-6

F32_SUBLANES = 8
LANES = 128
MXU_TILE = 256
VMEM_LIMIT_BYTES = 56 * 1024 * 1024

FFN_TT = 128
FFN_SPLIT = 2
FFN_CHUNK = 256
MIX_TL = 64
SCAN_LANES = 1024

f32 = jnp.float32
bf16 = jnp.bfloat16


def _rms(x, g):
    return x * lax.rsqrt(jnp.mean(x * x, axis=-1, keepdims=True) + RMS_EPS) * g


def _dot(a, b):
    return jnp.dot(a, b, preferred_element_type=f32)


def _load_rows(x_ref, slabs):
    if not slabs:
        return x_ref[...].reshape(-1, D_MODEL)
    tt = x_ref.shape[1] // F32_SUBLANES
    return jnp.concatenate(
        [jnp.concatenate([x_ref.at[j][pl.ds(b, tt, stride=F32_SUBLANES), :]
                          for j in range(D_MODEL // LANES)], axis=1)
         for b in range(F32_SUBLANES)], axis=0)


def _store_rows(o_ref, val, slabs):
    if not slabs:
        o_ref[...] = val.reshape(o_ref.shape)
        return
    tt = o_ref.shape[1] // F32_SUBLANES
    for b in range(F32_SUBLANES):
        for j in range(D_MODEL // LANES):
            o_ref.at[j][pl.ds(b, tt, stride=F32_SUBLANES), :] = (
                val[b * tt:(b + 1) * tt, j * LANES:(j + 1) * LANES])


def _ffn_kernel(x_ref, pre_ref, post_ref, wg_ref, wu_ref, wd_ref, o_ref, acc_ref, *,
                slabs_in, slabs_out):
    x_all = _load_rows(x_ref, slabs_in)
    n_split = FFN_SPLIT if x_all.shape[0] % (FFN_SPLIT * MXU_TILE) == 0 else 1
    group = x_all.shape[0] // n_split
    outs = []
    for s in range(n_split):
        rows = slice(s * group, (s + 1) * group)
        x = x_all[rows]
        n = _rms(x, pre_ref[...]).astype(bf16)
        for c in range(D_FF // FFN_CHUNK):
            cols = slice(c * FFN_CHUNK, (c + 1) * FFN_CHUNK)
            gate = _dot(n, wg_ref[:, cols])
            up = _dot(n, wu_ref[:, cols])
            act = (gate * jax.nn.sigmoid(gate) * up).astype(bf16)
            part = _dot(act, wd_ref[cols, :])
            if c == 0:
                acc_ref[rows, :] = part
            else:
                acc_ref[rows, :] += part
        outs.append(x + 0.5 * _rms(acc_ref[rows, :], post_ref[...]))
    _store_rows(o_ref, jnp.concatenate(outs, axis=0), slabs_out)


def _const_spec(shape):
    zeros = (0,) * len(shape)
    return pl.BlockSpec(shape, lambda *_: zeros, pipeline_mode=pl.Buffered(1))


def _ffn(x, grid, in_spec, out_spec, out_shape, rows, pre, post, wg, wu, wd, *,
         slabs_in=False, slabs_out=False):
    return pl.pallas_call(
        functools.partial(_ffn_kernel, slabs_in=slabs_in, slabs_out=slabs_out),
        grid=grid,
        in_specs=[
            in_spec,
            _const_spec((1, D_MODEL)),
            _const_spec((1, D_MODEL)),
            _const_spec((D_MODEL, D_FF)),
            _const_spec((D_MODEL, D_FF)),
            _const_spec((D_FF, D_MODEL)),
        ],
        out_specs=out_spec,
        out_shape=jax.ShapeDtypeStruct(out_shape, f32),
        scratch_shapes=[pltpu.VMEM((rows, D_MODEL), f32)],
        compiler_params=pltpu.CompilerParams(
            dimension_semantics=("arbitrary",) * len(grid),
            vmem_limit_bytes=VMEM_LIMIT_BYTES),
        name="ffn",
    )(x, pre, post, wg, wu, wd)


def _gelu_tanh(y):
    return 0.5 * y * (1.0 + jnp.tanh(math.sqrt(2.0 / math.pi) * (y + 0.044715 * (y * y * y))))


def _mixer_kernel(h_ref, hm_ref, gpre_ref, win_ref, bre_ref, bim_ref, are_ref, aim_ref,
                  cre_ref, cimn_ref, d_ref, wglu_ref, gssm_ref, pw_ref, pscale_ref, gpool_ref,
                  wout_ref, gpost_ref, o_ref,
                  xr_ref, xi_ref, bur_ref, bui_ref, xsr_ref, xsi_ref, ext_ref):
    rows = MIX_TL * F32_SUBLANES
    hist = N_META * F32_SUBLANES

    def project(h):
        return _dot(_rms(h, gpre_ref[...]).astype(bf16), win_ref[...])

    def drive(u_ssm, n):
        ub = u_ssm.astype(bf16)
        ch_per_tile = MXU_TILE // SSM_STATE * SSM_GROUP_CH
        for j in range(N_STATE // MXU_TILE):
            k0 = j * ch_per_tile // LANES * LANES
            cols = slice(j * MXU_TILE, (j + 1) * MXU_TILE)
            bur_ref[0:n, cols] = _dot(ub[:, k0:k0 + LANES], bre_ref[k0:k0 + LANES, cols])
            bui_ref[0:n, cols] = _dot(ub[:, k0:k0 + LANES], bim_ref[k0:k0 + LANES, cols])

    def scan(steps):
        for c in range(N_STATE // SCAN_LANES):
            lanes = slice(c * SCAN_LANES, (c + 1) * SCAN_LANES)
            ar = are_ref[:, lanes]
            ai = aim_ref[:, lanes]

            def body(tp, carry):
                xr, xi = carry
                r0 = pl.multiple_of(tp * (2 * F32_SUBLANES), 2 * F32_SUBLANES)
                r1 = r0 + F32_SUBLANES
                xr1 = ar * xr - ai * xi + bur_ref[pl.ds(r0, F32_SUBLANES), lanes]
                xi1 = ar * xi + ai * xr + bui_ref[pl.ds(r0, F32_SUBLANES), lanes]
                xr2 = ar * xr1 - ai * xi1 + bur_ref[pl.ds(r1, F32_SUBLANES), lanes]
                xi2 = ar * xi1 + ai * xr1 + bui_ref[pl.ds(r1, F32_SUBLANES), lanes]
                xsr_ref[pl.ds(r0, 2 * F32_SUBLANES), lanes] = (
                    jnp.concatenate([xr1, xr2], axis=0).astype(bf16))
                xsi_ref[pl.ds(r0, 2 * F32_SUBLANES), lanes] = (
                    jnp.concatenate([xi1, xi2], axis=0).astype(bf16))
                return xr2, xi2

            xr, xi = lax.fori_loop(0, steps // 2, body, (xr_ref[:, lanes], xi_ref[:, lanes]))
            xr_ref[:, lanes] = xr
            xi_ref[:, lanes] = xi

    @pl.when(pl.program_id(0) == 0)
    def _():
        xr_ref[...] = jnp.zeros_like(xr_ref)
        xi_ref[...] = jnp.zeros_like(xi_ref)
        pm = project(hm_ref[...])
        drive(pm[:, :SSM_WIDTH], hist)
        scan(N_META)
        ext_ref[0:hist, :] = pm[:, SSM_WIDTH:]

    h = jnp.concatenate([h_ref[j] for j in range(D_MODEL // LANES)], axis=1)
    p = project(h)
    u_ssm = p[:, :SSM_WIDTH]

    drive(u_ssm, rows)
    scan(MIX_TL)
    st_per_tile = MXU_TILE // SSM_GROUP_CH * SSM_STATE
    y_tiles = []
    for j in range(SSM_WIDTH // MXU_TILE):
        st = slice(j * st_per_tile, (j + 1) * st_per_tile)
        cols = slice(j * MXU_TILE, (j + 1) * MXU_TILE)
        y_tiles.append(_dot(xsr_ref[:, st], cre_ref[st, cols]) + _dot(xsi_ref[:, st], cimn_ref[st, cols]))
    y = jnp.concatenate(y_tiles, axis=1) + d_ref[...] * u_ssm
    gl = _gelu_tanh(y).astype(bf16)
    z_tiles = []
    for j in range(2 * SSM_WIDTH // MXU_TILE):
        k0 = j * MXU_TILE % SSM_WIDTH
        cols = slice(j * MXU_TILE, (j + 1) * MXU_TILE)
        z_tiles.append(_dot(gl[:, k0:k0 + MXU_TILE], wglu_ref[k0:k0 + MXU_TILE, cols]))
    n_val = SSM_WIDTH // MXU_TILE
    s = (jnp.concatenate(z_tiles[:n_val], axis=1)
         * jax.nn.sigmoid(jnp.concatenate(z_tiles[n_val:], axis=1)))
    ssm_n = _rms(s, gssm_ref[...]).astype(bf16)

    ext_ref[hist:hist + rows, :] = p[:, SSM_WIDTH:]
    pooled_groups = []
    for gi, w in enumerate(POOL_WINDOWS):
        lanes = slice(gi * POOL_GROUP, (gi + 1) * POOL_GROUP)
        cur = ext_ref[hist:hist + rows, lanes]
        acc = cur
        for k in range(1, w):
            lo = hist - k * F32_SUBLANES
            acc = acc + ext_ref[lo:lo + rows, lanes]
        pooled = acc * (1.0 / w) - cur
        pooled_groups.append(_dot(pooled.astype(bf16), pw_ref[gi]) * pscale_ref[:, lanes])
    ext_ref[0:hist, :] = ext_ref[rows:rows + hist, :]
    pool_n = _rms(jnp.concatenate(pooled_groups, axis=1), gpool_ref[...]).astype(bf16)

    mixed = _dot(ssm_n, wout_ref[0:SSM_WIDTH, :]) + _dot(pool_n, wout_ref[SSM_WIDTH:, :])
    out = h + _rms(mixed, gpost_ref[...])
    for j in range(D_MODEL // LANES):
        o_ref[j] = out[:, j * LANES:(j + 1) * LANES]


def _mixer(h1t, h1m, consts, n_steps):
    rows = MIX_TL * F32_SUBLANES
    hist = N_META * F32_SUBLANES
    tile = pl.BlockSpec((D_MODEL // LANES, rows, LANES), lambda i: (0, i, 0))
    return pl.pallas_call(
        _mixer_kernel,
        grid=(n_steps,),
        in_specs=[tile, _const_spec(h1m.shape)] + [_const_spec(c.shape) for c in consts],
        out_specs=tile,
        out_shape=jax.ShapeDtypeStruct(h1t.shape, f32),
        scratch_shapes=[
            pltpu.VMEM((F32_SUBLANES, N_STATE), f32),
            pltpu.VMEM((F32_SUBLANES, N_STATE), f32),
            pltpu.VMEM((rows, N_STATE), f32),
            pltpu.VMEM((rows, N_STATE), f32),
            pltpu.VMEM((rows, N_STATE), bf16),
            pltpu.VMEM((rows, N_STATE), bf16),
            pltpu.VMEM((hist + rows, POOL_WIDTH), f32),
        ],
        compiler_params=pltpu.CompilerParams(
            dimension_semantics=("arbitrary",),
            vmem_limit_bytes=VMEM_LIMIT_BYTES),
        name="mixer",
    )(h1t, h1m, *consts)


def _ssm_constants(lam_re, lam_im, log_dt, b_re, b_im, c_re, c_im, d_skip, w_glu):
    G, N, H = SSM_GROUPS, SSM_STATE, SSM_GROUP_CH
    dt = jnp.exp(log_dt)[:, None]
    decay = jnp.exp(lam_re * dt)
    ang = lam_im * dt
    a_re = decay * jnp.cos(ang)
    a_im = decay * jnp.sin(ang)
    nr = a_re - 1.0
    den = lam_re * lam_re + lam_im * lam_im
    q_re = (nr * lam_re + a_im * lam_im) / den
    q_im = (a_im * lam_re - nr * lam_im) / den
    bb_re = q_re[..., None] * b_re - q_im[..., None] * b_im
    bb_im = q_re[..., None] * b_im + q_im[..., None] * b_re
    eye = jnp.eye(G, dtype=f32)
    bre = jnp.einsum('gnh,gk->ghkn', bb_re, eye).reshape(G * H, G * N).astype(bf16)
    bim = jnp.einsum('gnh,gk->ghkn', bb_im, eye).reshape(G * H, G * N).astype(bf16)
    cre = jnp.einsum('ghn,gk->gnkh', c_re, eye).reshape(G * N, G * H).astype(bf16)
    cimn = jnp.einsum('ghn,gk->gnkh', -c_im, eye).reshape(G * N, G * H).astype(bf16)
    wv = jnp.einsum('ghj,gk->ghkj', w_glu[..., :H], eye).reshape(G * H, G * H)
    wgate = jnp.einsum('ghj,gk->ghkj', w_glu[..., H:], eye).reshape(G * H, G * H)
    wglu = jnp.concatenate([wv, wgate], axis=1).astype(bf16)
    are = jnp.broadcast_to(a_re.reshape(1, G * N), (F32_SUBLANES, G * N))
    aim = jnp.broadcast_to(a_im.reshape(1, G * N), (F32_SUBLANES, G * N))
    return bre, bim, are, aim, cre, cimn, d_skip.reshape(1, SSM_WIDTH), wglu


def kernel(x, meta_tokens, ffn1_pre_norm, ffn1_post_norm, ffn1_w_gate, ffn1_w_up, ffn1_w_down,
           mix_pre_norm, mix_post_norm, w_in, ssm_lambda_re, ssm_lambda_im, ssm_log_dt,
           ssm_b_re, ssm_b_im, ssm_c_re, ssm_c_im, ssm_d, ssm_w_glu, pool_w, pool_scale,
           ssm_out_norm, pool_out_norm, w_out, ffn2_pre_norm, ffn2_post_norm,
           ffn2_w_gate, ffn2_w_up, ffn2_w_down):
    B, L, D = x.shape
    assert D == D_MODEL and B == F32_SUBLANES and L % FFN_TT == 0 and L % MIX_TL == 0
    assert ffn1_pre_norm.shape[0] == 1, "single layer"
    assert N_META >= max(POOL_WINDOWS)
    row = lambda v: v.reshape(1, -1).astype(f32)

    ffn1 = (row(ffn1_pre_norm), row(ffn1_post_norm), ffn1_w_gate[0].astype(bf16),
            ffn1_w_up[0].astype(bf16), ffn1_w_down[0].astype(bf16))
    ffn2 = (row(ffn2_pre_norm), row(ffn2_post_norm), ffn2_w_gate[0].astype(bf16),
            ffn2_w_up[0].astype(bf16), ffn2_w_down[0].astype(bf16))

    rows = FFN_TT * B
    grid = (L // FFN_TT,)
    batch_major = pl.BlockSpec((B, FFN_TT, D), lambda i: (0, i, 0))
    slab_shape = (D // LANES, L * B, LANES)
    time_major = pl.BlockSpec((D // LANES, rows, LANES), lambda i: (0, i, 0))

    h1t = _ffn(x, grid, batch_major, time_major, slab_shape, rows, *ffn1, slabs_out=True)
    whole = pl.BlockSpec((N_META, D), lambda i: (0, 0))
    h1m = _ffn(meta_tokens.astype(f32), (1,), whole, whole, (N_META, D), N_META, *ffn1)
    h1m = jnp.repeat(h1m, B, axis=0)

    consts = (row(mix_pre_norm), w_in[0].astype(bf16)) + _ssm_constants(
        ssm_lambda_re[0], ssm_lambda_im[0], ssm_log_dt[0], ssm_b_re[0], ssm_b_im[0],
        ssm_c_re[0], ssm_c_im[0], ssm_d[0], ssm_w_glu[0]) + (
        row(ssm_out_norm), pool_w[0].astype(bf16), row(pool_scale), row(pool_out_norm),
        w_out[0].astype(bf16), row(mix_post_norm))
    h2t = _mixer(h1t, h1m, consts, L // MIX_TL)

    return _ffn(h2t, grid, time_major, batch_major, (B, L, D), rows, *ffn2, slabs_in=True)
```

```python
import functools
import math

import jax
import jax.numpy as jnp
from jax import lax
from jax.experimental import pallas as pl
from jax.experimental.pallas import tpu as pltpu

D_MODEL = 1024
N_META = 16
SSM_WIDTH = 512
POOL_WIDTH = 512
SSM_GROUP_CH = 16
SSM_GROUPS = SSM_WIDTH // SSM_GROUP_CH
SSM_STATE = 64
N_STATE = SSM_GROUPS * SSM_STATE
POOL_WINDOWS = (2, 4, 8, 16)
POOL_GROUP = POOL_WIDTH // len(POOL_WINDOWS)
D_FF = 2816
RMS_EPS = 1e-6

F32_SUBLANES = 8
LANES = 128
MXU_TILE = 256
VMEM_LIMIT_BYTES = 60 * 1024 * 1024

FFN_TT = 128
FFN_SPLIT = 2
FFN_CHUNK = 256
MIX_TL = 64
MIX_GROUPS = 2
SCAN_LANES = 1024

f32 = jnp.float32
bf16 = jnp.bfloat16


def _rms(x, g):
    return x * lax.rsqrt(jnp.mean(x * x, axis=-1, keepdims=True) + RMS_EPS) * g


def _dot(a, b):
    return jnp.dot(a, b, preferred_element_type=f32)


def _load_rows(x_ref, slabs):
    if not slabs:
        return x_ref[...].reshape(-1, D_MODEL)
    tt = x_ref.shape[1] // F32_SUBLANES
    return jnp.concatenate(
        [jnp.concatenate([x_ref.at[j][pl.ds(b, tt, stride=F32_SUBLANES), :]
                          for j in range(D_MODEL // LANES)], axis=1)
         for b in range(F32_SUBLANES)], axis=0)


def _store_rows(o_ref, val, slabs):
    if not slabs:
        o_ref[...] = val.reshape(o_ref.shape)
        return
    tt = o_ref.shape[1] // F32_SUBLANES
    for b in range(F32_SUBLANES):
        for j in range(D_MODEL // LANES):
            o_ref.at[j][pl.ds(b, tt, stride=F32_SUBLANES), :] = (
                val[b * tt:(b + 1) * tt, j * LANES:(j + 1) * LANES])


def _ffn_kernel(x_ref, pre_ref, post_ref, wg_ref, wu_ref, wd_ref, o_ref, acc_ref, *,
                slabs_in, slabs_out):
    x_all = _load_rows(x_ref, slabs_in)
    n_split = FFN_SPLIT if x_all.shape[0] % (FFN_SPLIT * MXU_TILE) == 0 else 1
    group = x_all.shape[0] // n_split
    outs = []
    for s in range(n_split):
        rows = slice(s * group, (s + 1) * group)
        x = x_all[rows]
        n = _rms(x, pre_ref[...]).astype(bf16)
        for c in range(D_FF // FFN_CHUNK):
            cols = slice(c * FFN_CHUNK, (c + 1) * FFN_CHUNK)
            gate = _dot(n, wg_ref[:, cols])
            up = _dot(n, wu_ref[:, cols])
            act = (gate * jax.nn.sigmoid(gate) * up).astype(bf16)
            part = _dot(act, wd_ref[cols, :])
            if c == 0:
                acc_ref[rows, :] = part
            else:
                acc_ref[rows, :] += part
        outs.append(x + 0.5 * _rms(acc_ref[rows, :], post_ref[...]))
    _store_rows(o_ref, jnp.concatenate(outs, axis=0), slabs_out)


def _const_spec(shape):
    zeros = (0,) * len(shape)
    return pl.BlockSpec(shape, lambda *_: zeros, pipeline_mode=pl.Buffered(1))


def _ffn(x, grid, in_spec, out_spec, out_shape, rows, pre, post, wg, wu, wd, *,
         slabs_in=False, slabs_out=False):
    return pl.pallas_call(
        functools.partial(_ffn_kernel, slabs_in=slabs_in, slabs_out=slabs_out),
        grid=grid,
        in_specs=[
            in_spec,
            _const_spec((1, D_MODEL)),
            _const_spec((1, D_MODEL)),
            _const_spec((D_MODEL, D_FF)),
            _const_spec((D_MODEL, D_FF)),
            _const_spec((D_FF, D_MODEL)),
        ],
        out_specs=out_spec,
        out_shape=jax.ShapeDtypeStruct(out_shape, f32),
        scratch_shapes=[pltpu.VMEM((rows, D_MODEL), f32)],
        compiler_params=pltpu.CompilerParams(
            dimension_semantics=("arbitrary",) * len(grid),
            vmem_limit_bytes=VMEM_LIMIT_BYTES),
        name="ffn",
    )(x, pre, post, wg, wu, wd)


def _gelu_tanh(y):
    return 0.5 * y * (1.0 + jnp.tanh(math.sqrt(2.0 / math.pi) * (y + 0.044715 * (y * y * y))))


def _mixer_kernel(h_ref, hm_ref, gpre_ref, win_ref, bre_ref, bim_ref, are_ref, aim_ref,
                  cre_ref, cimn_ref, d_ref, wglu_ref, gssm_ref, pw_ref, pscale_ref, gpool_ref,
                  wout_ref, gpost_ref, o_ref,
                  xr_ref, xi_ref, bur_ref, bui_ref, xsr_ref, xsi_ref, ext_ref):
    rows = MIX_TL * F32_SUBLANES
    hist = N_META * F32_SUBLANES

    def project(h):
        return _dot(_rms(h, gpre_ref[...]).astype(bf16), win_ref[...])

    def drive(u_ssm, n):
        ub = u_ssm.astype(bf16)
        ch_per_tile = MXU_TILE // SSM_STATE * SSM_GROUP_CH
        for j in range(N_STATE // MXU_TILE):
            k0 = j * ch_per_tile // LANES * LANES
            cols = slice(j * MXU_TILE, (j + 1) * MXU_TILE)
            bur_ref[0:n, cols] = _dot(ub[:, k0:k0 + LANES], bre_ref[k0:k0 + LANES, cols])
            bui_ref[0:n, cols] = _dot(ub[:, k0:k0 + LANES], bim_ref[k0:k0 + LANES, cols])

    def scan(steps, base, unroll=False):
        for c in range(N_STATE // SCAN_LANES):
            lanes = slice(c * SCAN_LANES, (c + 1) * SCAN_LANES)
            ar = are_ref[:, lanes]
            ai = aim_ref[:, lanes]

            def body(tp, carry):
                xr, xi = carry
                r0 = pl.multiple_of(tp * (2 * F32_SUBLANES), 2 * F32_SUBLANES)
                r1 = r0 + F32_SUBLANES
                w0 = pl.multiple_of(base + tp * (2 * F32_SUBLANES), 2 * F32_SUBLANES)
                xr1 = ar * xr - ai * xi + bur_ref[pl.ds(r0, F32_SUBLANES), lanes]
                xi1 = ar * xi + ai * xr + bui_ref[pl.ds(r0, F32_SUBLANES), lanes]
                xr2 = ar * xr1 - ai * xi1 + bur_ref[pl.ds(r1, F32_SUBLANES), lanes]
                xi2 = ar * xi1 + ai * xr1 + bui_ref[pl.ds(r1, F32_SUBLANES), lanes]
                xsr_ref[pl.ds(w0, 2 * F32_SUBLANES), lanes] = (
                    jnp.concatenate([xr1, xr2], axis=0).astype(bf16))
                xsi_ref[pl.ds(w0, 2 * F32_SUBLANES), lanes] = (
                    jnp.concatenate([xi1, xi2], axis=0).astype(bf16))
                return xr2, xi2

            xr, xi = lax.fori_loop(0, steps // 2, body, (xr_ref[:, lanes], xi_ref[:, lanes]),
                                   unroll=unroll)
            xr_ref[:, lanes] = xr
            xi_ref[:, lanes] = xi

    @pl.when(pl.program_id(0) == 0)
    def _():
        xr_ref[...] = jnp.zeros_like(xr_ref)
        xi_ref[...] = jnp.zeros_like(xi_ref)
        pm = project(hm_ref[...])
        drive(pm[:, :SSM_WIDTH], hist)
        scan(N_META, 0)
        ext_ref[0:hist, :] = pm[:, SSM_WIDTH:]

    for grp in range(MIX_GROUPS):
        base = grp * rows
        h = jnp.concatenate([h_ref[j, base:base + rows, :] for j in range(D_MODEL // LANES)],
                            axis=1)
        p = project(h)
        u_ssm = p[:, :SSM_WIDTH]

        drive(u_ssm, rows)
        scan(MIX_TL, base, unroll=True)
        st_per_tile = MXU_TILE // SSM_GROUP_CH * SSM_STATE
        y_tiles = []
        for j in range(SSM_WIDTH // MXU_TILE):
            st = slice(j * st_per_tile, (j + 1) * st_per_tile)
            cols = slice(j * MXU_TILE, (j + 1) * MXU_TILE)
            y_tiles.append(_dot(xsr_ref[base:base + rows, st], cre_ref[st, cols])
                           + _dot(xsi_ref[base:base + rows, st], cimn_ref[st, cols]))
        y = jnp.concatenate(y_tiles, axis=1) + d_ref[...] * u_ssm
        gl = _gelu_tanh(y).astype(bf16)
        z_tiles = []
        for j in range(2 * SSM_WIDTH // MXU_TILE):
            k0 = j * MXU_TILE % SSM_WIDTH
            cols = slice(j * MXU_TILE, (j + 1) * MXU_TILE)
            z_tiles.append(_dot(gl[:, k0:k0 + MXU_TILE], wglu_ref[k0:k0 + MXU_TILE, cols]))
        n_val = SSM_WIDTH // MXU_TILE
        s = (jnp.concatenate(z_tiles[:n_val], axis=1)
             * jax.nn.sigmoid(jnp.concatenate(z_tiles[n_val:], axis=1)))
        ssm_n = _rms(s, gssm_ref[...]).astype(bf16)

        cur0 = hist + base
        ext_ref[cur0:cur0 + rows, :] = p[:, SSM_WIDTH:]
        pooled_groups = []
        for gi, w in enumerate(POOL_WINDOWS):
            lanes = slice(gi * POOL_GROUP, (gi + 1) * POOL_GROUP)
            cur = ext_ref[cur0:cur0 + rows, lanes]
            acc = cur
            for k in range(1, w):
                lo = cur0 - k * F32_SUBLANES
                acc = acc + ext_ref[lo:lo + rows, lanes]
            pooled = acc * (1.0 / w) - cur
            pooled_groups.append(_dot(pooled.astype(bf16), pw_ref[gi]) * pscale_ref[:, lanes])
        pool_n = _rms(jnp.concatenate(pooled_groups, axis=1), gpool_ref[...]).astype(bf16)

        mixed = _dot(ssm_n, wout_ref[0:SSM_WIDTH, :]) + _dot(pool_n, wout_ref[SSM_WIDTH:, :])
        out = h + _rms(mixed, gpost_ref[...])
        for j in range(D_MODEL // LANES):
            o_ref[j, base:base + rows, :] = out[:, j * LANES:(j + 1) * LANES]
    ext_ref[0:hist, :] = ext_ref[MIX_GROUPS * rows:MIX_GROUPS * rows + hist, :]


def _mixer(h1t, h1m, consts, n_steps):
    rows = MIX_TL * F32_SUBLANES
    hist = N_META * F32_SUBLANES
    tile = pl.BlockSpec((D_MODEL // LANES, MIX_GROUPS * rows, LANES), lambda i: (0, i, 0))
    return pl.pallas_call(
        _mixer_kernel,
        grid=(n_steps,),
        in_specs=[tile, _const_spec(h1m.shape)] + [_const_spec(c.shape) for c in consts],
        out_specs=tile,
        out_shape=jax.ShapeDtypeStruct(h1t.shape, f32),
        scratch_shapes=[
            pltpu.VMEM((F32_SUBLANES, N_STATE), f32),
            pltpu.VMEM((F32_SUBLANES, N_STATE), f32),
            pltpu.VMEM((rows, N_STATE), f32),
            pltpu.VMEM((rows, N_STATE), f32),
            pltpu.VMEM((MIX_GROUPS * rows, N_STATE), bf16),
            pltpu.VMEM((MIX_GROUPS * rows, N_STATE), bf16),
            pltpu.VMEM((hist + MIX_GROUPS * rows, POOL_WIDTH), f32),
        ],
        compiler_params=pltpu.CompilerParams(
            dimension_semantics=("arbitrary",),
            vmem_limit_bytes=VMEM_LIMIT_BYTES),
        name="mixer",
    )(h1t, h1m, *consts)


def _ssm_constants(lam_re, lam_im, log_dt, b_re, b_im, c_re, c_im, d_skip, w_glu):
    G, N, H = SSM_GROUPS, SSM_STATE, SSM_GROUP_CH
    dt = jnp.exp(log_dt)[:, None]
    decay = jnp.exp(lam_re * dt)
    ang = lam_im * dt
    a_re = decay * jnp.cos(ang)
    a_im = decay * jnp.sin(ang)
    nr = a_re - 1.0
    den = lam_re * lam_re + lam_im * lam_im
    q_re = (nr * lam_re + a_im * lam_im) / den
    q_im = (a_im * lam_re - nr * lam_im) / den
    bb_re = q_re[..., None] * b_re - q_im[..., None] * b_im
    bb_im = q_re[..., None] * b_im + q_im[..., None] * b_re

    def block_diag(blocks_2d, reps, row_block, col_block):
        full = jnp.tile(blocks_2d, reps)
        r = lax.broadcasted_iota(jnp.int32, full.shape, 0) // row_block
        c = lax.broadcasted_iota(jnp.int32, full.shape, 1) // col_block
        return jnp.where(r == c, full, 0.0).astype(bf16)

    bre = block_diag(jnp.transpose(bb_re, (2, 0, 1)).reshape(H, G * N), (G, 1), H, N)
    bim = block_diag(jnp.transpose(bb_im, (2, 0, 1)).reshape(H, G * N), (G, 1), H, N)
    cre = block_diag(jnp.transpose(c_re, (0, 2, 1)).reshape(G * N, H), (1, G), N, H)
    cimn = block_diag(jnp.transpose(-c_im, (0, 2, 1)).reshape(G * N, H), (1, G), N, H)
    wv = block_diag(w_glu[..., :H].reshape(G * H, H), (1, G), H, H)
    wgate = block_diag(w_glu[..., H:].reshape(G * H, H), (1, G), H, H)
    wglu = jnp.concatenate([wv, wgate], axis=1)
    are = jnp.broadcast_to(a_re.reshape(1, G * N), (F32_SUBLANES, G * N))
    aim = jnp.broadcast_to(a_im.reshape(1, G * N), (F32_SUBLANES, G * N))
    return bre, bim, are, aim, cre, cimn, d_skip.reshape(1, SSM_WIDTH), wglu


def kernel(x, meta_tokens, ffn1_pre_norm, ffn1_post_norm, ffn1_w_gate, ffn1_w_up, ffn1_w_down,
           mix_pre_norm, mix_post_norm, w_in, ssm_lambda_re, ssm_lambda_im, ssm_log_dt,
           ssm_b_re, ssm_b_im, ssm_c_re, ssm_c_im, ssm_d, ssm_w_glu, pool_w, pool_scale,
           ssm_out_norm, pool_out_norm, w_out, ffn2_pre_norm, ffn2_post_norm,
           ffn2_w_gate, ffn2_w_up, ffn2_w_down):
    B, L, D = x.shape
    assert D == D_MODEL and B == F32_SUBLANES
    assert L % FFN_TT == 0 and L % (MIX_TL * MIX_GROUPS) == 0
    assert ffn1_pre_norm.shape[0] == 1, "single layer"
    assert N_META >= max(POOL_WINDOWS)
    row = lambda v: v.reshape(1, -1).astype(f32)

    ffn1 = (row(ffn1_pre_norm), row(ffn1_post_norm), ffn1_w_gate[0].astype(bf16),
            ffn1_w_up[0].astype(bf16), ffn1_w_down[0].astype(bf16))
    ffn2 = (row(ffn2_pre_norm), row(ffn2_post_norm), ffn2_w_gate[0].astype(bf16),
            ffn2_w_up[0].astype(bf16), ffn2_w_down[0].astype(bf16))

    rows = FFN_TT * B
    grid = (L // FFN_TT,)
    batch_major = pl.BlockSpec((B, FFN_TT, D), lambda i: (0, i, 0))
    slab_shape = (D // LANES, L * B, LANES)
    time_major = pl.BlockSpec((D // LANES, rows, LANES), lambda i: (0, i, 0))

    h1t = _ffn(x, grid, batch_major, time_major, slab_shape, rows, *ffn1, slabs_out=True)
    whole = pl.BlockSpec((N_META, D), lambda i: (0, 0))
    h1m = _ffn(meta_tokens.astype(f32), (1,), whole, whole, (N_META, D), N_META, *ffn1)
    h1m = jnp.repeat(h1m, B, axis=0)

    consts = (row(mix_pre_norm), w_in[0].astype(bf16)) + _ssm_constants(
        ssm_lambda_re[0], ssm_lambda_im[0], ssm_log_dt[0], ssm_b_re[0], ssm_b_im[0],
        ssm_c_re[0], ssm_c_im[0], ssm_d[0], ssm_w_glu[0]) + (
        row(ssm_out_norm), pool_w[0].astype(bf16), row(pool_scale), row(pool_out_norm),
        w_out[0].astype(bf16), row(mix_post_norm))
    h2t = _mixer(h1t, h1m, consts, L // (MIX_TL * MIX_GROUPS))

    return _ffn(h2t, grid, time_major, batch_major, (B, L, D), rows, *ffn2, slabs_in=True)
```
